```python
import math
import jax, jax.numpy as jnp
from jax import lax
import numpy as np

D_MODEL = 1024
BATCH = 16
SEQ = 4096
DEPTH = 1
DEC_BATCH = 8
DEC_SEQ = 4096
PAST_LEN = 128

MIX_W = D_MODEL
POOL_W = MIX_W // 2
N_POOL = 4
POOL_GROUP = POOL_W // N_POOL
POOL_WINDOWS = (2, 4, 8, 16)
HEAD_DIM = 64
N_HEADS = (MIX_W - POOL_W) // HEAD_DIM
N_KV = 2
GROUP = N_HEADS // N_KV
Q_W = N_HEADS * HEAD_DIM
KV_W = N_KV * HEAD_DIM
IN_W = POOL_W + Q_W + 2 * KV_W
AX_DIM = HEAD_DIM // 2
ROPE_THETA = 10000.0
GRID_W = 64
Q_BLOCK = 128
D_FF = ((8 * D_MODEL // 3 + 255) // 256) * 256
PLE_DIM = 256
EPS = 1e-6

kernel_name = "hybrid_pool_gqa_axial_encoder"


def rms_norm(x, g):
    xf = x.astype(jnp.float32)
    y = xf * lax.rsqrt(jnp.mean(xf * xf, axis=-1, keepdims=True) + EPS)
    return (y * g.astype(jnp.float32)).astype(x.dtype)


def axial_rope_tables(seq):
    rows = seq // GRID_W
    row = jnp.repeat(jnp.arange(rows, dtype=jnp.float32), GRID_W)
    col = jnp.tile(jnp.arange(GRID_W, dtype=jnp.float32), rows)
    inv = 1.0 / (ROPE_THETA ** (jnp.arange(0, AX_DIM, 2, dtype=jnp.float32) / AX_DIM))
    ang = jnp.concatenate([row[:, None] * inv, col[:, None] * inv], axis=-1)
    return jnp.cos(ang), jnp.sin(ang)


def apply_rope(x, cos, sin):
    xf = x.astype(jnp.float32).reshape(*x.shape[:-1], HEAD_DIM // 2, 2)
    x0, x1 = xf[..., 0], xf[..., 1]
    c = cos[None, :, None, :]
    s = sin[None, :, None, :]
    out = jnp.stack([x0 * c - x1 * s, x0 * s + x1 * c], axis=-1)
    return out.reshape(x.shape).astype(x.dtype)


def pool_mixer(u, pool_w, pool_scale):
    B, S, _ = u.shape
    uf = u.astype(jnp.float32).reshape(B, S, N_POOL, POOL_GROUP)
    cs = jnp.concatenate([jnp.zeros((B, 1, N_POOL, POOL_GROUP), jnp.float32),
                          jnp.cumsum(uf, axis=1)], axis=1)
    t = jnp.arange(S)
    outs = []
    for g, w in enumerate(POOL_WINDOWS):
        lo = jnp.clip(t - w // 2, 0, S - 1)
        hi = jnp.clip(t + (w - 1 - w // 2), 0, S - 1)
        win = cs[:, hi + 1, g] - cs[:, lo, g]
        cnt = (hi - lo + 1).astype(jnp.float32)[None, :, None]
        outs.append(win / cnt - uf[:, :, g])
    m = jnp.stack(outs, axis=2).astype(u.dtype)
    y = jnp.einsum('bsgc,gcd->bsgd', m, pool_w)
    return y.reshape(B, S, POOL_W) * pool_scale


def block_attention(q, k, v):
    B, S, _, _ = q.shape
    nblk = S // Q_BLOCK
    scale = 1.0 / math.sqrt(HEAD_DIM)
    qb = q.reshape(B, nblk, Q_BLOCK, N_KV, GROUP, HEAD_DIM).transpose(1, 0, 2, 3, 4, 5)

    def one_block(qblk):
        s = jnp.einsum('bqkgd,bskd->bkgqs', qblk, k, preferred_element_type=jnp.float32) * scale
        p = jax.nn.softmax(s, axis=-1).astype(v.dtype)
        return jnp.einsum('bkgqs,bskd->bqkgd', p, v)

    o = lax.map(one_block, qb)
    return o.transpose(1, 0, 2, 3, 4, 5).reshape(B, S, Q_W)


def encoder_layer(x, p, ln_mix_pre, ln_mix_post, w_in, pool_w, pool_scale, q_norm, k_norm, w_out,
                  ln_ffn_pre, ln_ffn_post, w_gate, w_up, w_down, w_ple_proj, w_ple_gate, b_ple_gate):
    B, S, _ = x.shape
    cos, sin = axial_rope_tables(S)
    h = rms_norm(x, ln_mix_pre)
    z = h @ w_in
    u = z[..., :POOL_W]
    q = z[..., POOL_W:POOL_W + Q_W].reshape(B, S, N_HEADS, HEAD_DIM)
    k = z[..., POOL_W + Q_W:POOL_W + Q_W + KV_W].reshape(B, S, N_KV, HEAD_DIM)
    v = z[..., POOL_W + Q_W + KV_W:].reshape(B, S, N_KV, HEAD_DIM)
    pool_out = pool_mixer(u, pool_w, pool_scale)
    q = apply_rope(rms_norm(q, q_norm), cos, sin)
    k = apply_rope(rms_norm(k, k_norm), cos, sin)
    attn_out = block_attention(q, k, v)
    mix = jnp.concatenate([pool_out, attn_out], axis=-1) @ w_out
    x = x + rms_norm(mix, ln_mix_post)
    h = rms_norm(x, ln_ffn_pre)
    f = (jax.nn.silu(h @ w_gate) * (h @ w_up)) @ w_down
    x = x + rms_norm(f, ln_ffn_post)
    e = p @ w_ple_proj
    g = jax.nn.sigmoid((x @ w_ple_gate + b_ple_gate).astype(jnp.float32)).astype(x.dtype)
    return x + e * g


def setup_inputs(seed: int = 0) -> dict:
    key = jax.random.key(seed)
    ks = jax.random.split(key, 24)
    f32 = jnp.float32

    def nrm(k, shape, scale):
        return jax.random.normal(k, shape, f32) * scale

    def gain(k, shape):
        return 1.0 + 0.05 * jax.random.normal(k, shape, f32)

    return {
        "x_prompt": nrm(ks[0], (BATCH, SEQ, D_MODEL), 1.0),
        "x_sample": nrm(ks[1], (DEC_BATCH, DEC_SEQ, D_MODEL), 1.0),
        "p_prompt": nrm(ks[2], (DEPTH, BATCH, SEQ, PLE_DIM), 1.0),
        "p_sample": nrm(ks[3], (DEPTH, DEC_BATCH, DEC_SEQ, PLE_DIM), 1.0),
        "ln_mix_pre": gain(ks[4], (DEPTH, D_MODEL)),
        "ln_mix_post": gain(ks[5], (DEPTH, D_MODEL)),
        "w_in": nrm(ks[6], (DEPTH, D_MODEL, IN_W), D_MODEL ** -0.5),
        "pool_w": nrm(ks[7], (DEPTH, N_POOL, POOL_GROUP, POOL_GROUP), POOL_GROUP ** -0.5),
        "pool_scale": gain(ks[8], (DEPTH, POOL_W)),
        "q_norm": gain(ks[9], (DEPTH, HEAD_DIM)),
        "k_norm": gain(ks[10], (DEPTH, HEAD_DIM)),
        "w_out": nrm(ks[11], (DEPTH, MIX_W, D_MODEL), MIX_W ** -0.5),
        "ln_ffn_pre": gain(ks[12], (DEPTH, D_MODEL)),
        "ln_ffn_post": gain(ks[13], (DEPTH, D_MODEL)),
        "w_gate": nrm(ks[14], (DEPTH, D_MODEL, D_FF), D_MODEL ** -0.5),
        "w_up": nrm(ks[15], (DEPTH, D_MODEL, D_FF), D_MODEL ** -0.5),
        "w_down": nrm(ks[16], (DEPTH, D_FF, D_MODEL), D_FF ** -0.5),
        "w_ple_proj": nrm(ks[17], (DEPTH, PLE_DIM, D_MODEL), PLE_DIM ** -0.5),
        "w_ple_gate": nrm(ks[18], (DEPTH, D_MODEL, D_MODEL), D_MODEL ** -0.5),
        "b_ple_gate": nrm(ks[19], (DEPTH, D_MODEL), 0.02),
    }


def reference(x_prompt, x_sample, p_prompt, p_sample, ln_mix_pre, ln_mix_post, w_in, pool_w, pool_scale,
              q_norm, k_norm, w_out, ln_ffn_pre, ln_ffn_post, w_gate, w_up, w_down,
              w_ple_proj, w_ple_gate, b_ple_gate):
    def trunk(x, p):
        for i in range(DEPTH):
            x = encoder_layer(x, p[i], ln_mix_pre[i], ln_mix_post[i], w_in[i], pool_w[i], pool_scale[i],
                              q_norm[i], k_norm[i], w_out[i], ln_ffn_pre[i], ln_ffn_post[i],
                              w_gate[i], w_up[i], w_down[i], w_ple_proj[i], w_ple_gate[i], b_ple_gate[i])
        return x

    y_prompt = trunk(x_prompt, p_prompt)
    y_sample = trunk(x_sample, p_sample)
    return (y_prompt, y_sample)
```

```python
import functools
import math

import jax
import jax.numpy as jnp
import numpy as np
from jax import lax
from jax.experimental import pallas as pl
from jax.experimental.pallas import tpu as pltpu

F32 = jnp.float32
BF16 = jnp.bfloat16

POOL_WINDOWS = (2, 4, 8, 16)
N_POOL = len(POOL_WINDOWS)
HALO = 8
HEAD_DIM = 64
HALF = HEAD_DIM // 2
N_KV = 2
GRID_W = 64
ROPE_THETA = 10000.0
EPS = 1e-6
Q_SCALE = (1.0 / math.sqrt(HEAD_DIM)) * math.log2(math.e)

V7X_VMEM_LIMIT_BYTES = 56 * 1024 * 1024


def _rms(x, g):
    ms = jnp.mean(x * x, axis=-1, keepdims=True)
    return x * lax.rsqrt(ms + EPS) * g


def _pre_kernel(x_ref, g_ref, wu_ref, wqkvt_ref, qg_ref, kg_ref, cos_ref, sin_ref,
                u_ref, qt_ref, k_ref, vt_ref, *, n_heads):
    h = _rms(x_ref[...], g_ref[...]).astype(BF16)
    u_ref[...] = jnp.dot(h, wu_ref[...], preferred_element_type=F32)
    zt = lax.dot_general(wqkvt_ref[...], h, (((1,), (1,)), ((), ())),
                         preferred_element_type=F32)
    cos = cos_ref[...]
    sin = sin_ref[...]

    def norm_rope(blk, g):
        ssq = jnp.sum(blk * blk, axis=0, keepdims=True)
        n = blk * lax.rsqrt(ssq * (1.0 / HEAD_DIM) + EPS) * g
        x0 = n[:HALF]
        x1 = n[HALF:]
        return jnp.concatenate([x0 * cos - x1 * sin, x0 * sin + x1 * cos], axis=0)

    qg = qg_ref[...]
    kg = kg_ref[...]
    for hd in range(n_heads):
        r = hd * HEAD_DIM
        qt_ref[r:r + HEAD_DIM, :] = (norm_rope(zt[r:r + HEAD_DIM], qg) * Q_SCALE).astype(BF16)
    q_w = n_heads * HEAD_DIM
    kt = jnp.concatenate(
        [norm_rope(zt[q_w + j * HEAD_DIM:q_w + (j + 1) * HEAD_DIM], kg) for j in range(N_KV)],
        axis=0)
    k_ref[...] = kt.T.astype(BF16)
    kv_w = N_KV * HEAD_DIM
    vt_ref[...] = zt[q_w + kv_w:q_w + 2 * kv_w].astype(BF16)


def _attn_kernel(qt_ref, k_ref, vt_ref, o_ref, *, n_heads):
    k = k_ref[...]
    group = n_heads // N_KV
    outs = []
    for hd in range(n_heads):
        j = hd // group
        qh = qt_ref[hd * HEAD_DIM:(hd + 1) * HEAD_DIM, :]
        parts = [jnp.zeros_like(qh)] * N_KV
        parts[j] = qh
        wq = jnp.concatenate(parts, axis=0)
        st = jnp.dot(k, wq, preferred_element_type=F32)
        m = jnp.max(st, axis=0, keepdims=True)
        p = jnp.exp2(st - m)
        l = jnp.sum(p, axis=0, keepdims=True)
        ot = jnp.dot(vt_ref[j * HEAD_DIM:(j + 1) * HEAD_DIM, :], p.astype(BF16),
                     preferred_element_type=F32)
        outs.append(ot / l)
    o_ref[...] = jnp.concatenate(outs, axis=0).T.astype(BF16)


def _post_kernel(x_ref, u_ref, uprev_ref, unext_ref, attn_ref, p_ref,
                 poolw_ref, pscale_ref, wout_ref, gmix_ref, gpre_ref, gpost_ref,
                 wgate_ref, wup_ref, wdown_ref, wproj_ref, wpg_ref, bpg_ref,
                 y_ref, uext_ref, *, seq_len):
    i = pl.program_id(1)
    tm = x_ref.shape[0]
    group_w = poolw_ref.shape[1]

    uext_ref[0:HALO, :] = jnp.where(i > 0, uprev_ref[...], 0.0)
    uext_ref[HALO:HALO + tm, :] = u_ref[...]
    uext_ref[HALO + tm:HALO + tm + HALO, :] = jnp.where(i < pl.num_programs(1) - 1, unext_ref[...], 0.0)

    t = lax.broadcasted_iota(jnp.int32, (tm, group_w), 0) + i * tm
    pooled = []
    for g, w in enumerate(POOL_WINDOWS):
        c0 = g * group_w
        back = w // 2
        fwd = w - 1 - back
        win = uext_ref[HALO - back:HALO - back + tm, c0:c0 + group_w]
        for d in range(-back + 1, fwd + 1):
            win = win + uext_ref[HALO + d:HALO + d + tm, c0:c0 + group_w]
        lo = jnp.maximum(t - back, 0)
        hi = jnp.minimum(t + fwd, seq_len - 1)
        cnt = (hi - lo + 1).astype(F32)
        m = win / cnt - uext_ref[HALO:HALO + tm, c0:c0 + group_w]
        pooled.append(jnp.dot(m.astype(BF16), poolw_ref[g], preferred_element_type=F32))
    pool_out = jnp.concatenate(pooled, axis=-1) * pscale_ref[...]

    cat = jnp.concatenate([pool_out.astype(BF16), attn_ref[...]], axis=-1)
    mix = jnp.dot(cat, wout_ref[...], preferred_element_type=F32)
    x1 = x_ref[...] + _rms(mix, gmix_ref[...])

    h = _rms(x1, gpre_ref[...]).astype(BF16)
    gate = jnp.dot(h, wgate_ref[...], preferred_element_type=F32)
    up = jnp.dot(h, wup_ref[...], preferred_element_type=F32)
    act = (gate * jax.nn.sigmoid(gate) * up).astype(BF16)
    f = jnp.dot(act, wdown_ref[...], preferred_element_type=F32)
    x2 = x1 + _rms(f, gpost_ref[...])

    e = jnp.dot(p_ref[...].astype(BF16), wproj_ref[...], preferred_element_type=F32)
    gt = jax.nn.sigmoid(jnp.dot(x2.astype(BF16), wpg_ref[...], preferred_element_type=F32)
                        + bpg_ref[...])
    y_ref[...] = x2 + e * gt


def _rope_tables_t(seq):
    rows = seq // GRID_W
    row = jnp.repeat(jnp.arange(rows, dtype=F32), GRID_W)
    col = jnp.tile(jnp.arange(GRID_W, dtype=F32), rows)
    inv = 1.0 / (ROPE_THETA ** (jnp.arange(0, HALF, 2, dtype=F32) / HALF))
    ang = jnp.concatenate([row[:, None] * inv, col[:, None] * inv], axis=-1)
    return jnp.cos(ang).T, jnp.sin(ang).T


def _const_spec(shape):
    nd = len(shape)
    return pl.BlockSpec(shape, lambda b, i: (0,) * nd, pipeline_mode=pl.Buffered(1))


def _tile(seq, want):
    tm = min(want, seq)
    assert seq % tm == 0 and tm % 128 == 0
    return tm


def _layer(x, p_all, layer, w, *, tm_pre=512, tq=256, tm_post=256):
    B, S, D = x.shape
    pool_w_total = w["wu"].shape[1]
    qkv_w = w["wqkvt"].shape[0]
    kv_w = N_KV * HEAD_DIM
    q_w = qkv_w - 2 * kv_w
    n_heads = q_w // HEAD_DIM
    assert S % GRID_W == 0 and S % HALO == 0
    cos_t, sin_t = _rope_tables_t(S)
    params = functools.partial(pltpu.CompilerParams, vmem_limit_bytes=V7X_VMEM_LIMIT_BYTES)

    tm = _tile(S, tm_pre)
    u, qt, k, vt = pl.pallas_call(
        functools.partial(_pre_kernel, n_heads=n_heads),
        grid=(B, S // tm),
        in_specs=[
            pl.BlockSpec((None, tm, D), lambda b, i: (b, i, 0)),
            _const_spec((1, D)),
            _const_spec((D, pool_w_total)),
            _const_spec((qkv_w, D)),
            _const_spec((HEAD_DIM, 1)),
            _const_spec((HEAD_DIM, 1)),
            pl.BlockSpec((HALF, tm), lambda b, i: (0, i)),
            pl.BlockSpec((HALF, tm), lambda b, i: (0, i)),
        ],
        out_specs=[
            pl.BlockSpec((None, tm, pool_w_total), lambda b, i: (b, i, 0)),
            pl.BlockSpec((None, q_w, tm), lambda b, i: (b, 0, i)),
            pl.BlockSpec((None, tm, kv_w), lambda b, i: (b, i, 0)),
            pl.BlockSpec((None, kv_w, tm), lambda b, i: (b, 0, i)),
        ],
        out_shape=[
            jax.ShapeDtypeStruct((B, S, pool_w_total), F32),
            jax.ShapeDtypeStruct((B, q_w, S), BF16),
            jax.ShapeDtypeStruct((B, S, kv_w), BF16),
            jax.ShapeDtypeStruct((B, kv_w, S), BF16),
        ],
        compiler_params=params(dimension_semantics=("arbitrary", "arbitrary")),
        name="pre",
    )(x, w["g_mix_pre"], w["wu"], w["wqkvt"], w["qg"], w["kg"], cos_t, sin_t)

    tq_ = _tile(S, tq)
    attn = pl.pallas_call(
        functools.partial(_attn_kernel, n_heads=n_heads),
        grid=(B, S // tq_),
        in_specs=[
            pl.BlockSpec((None, q_w, tq_), lambda b, i: (b, 0, i)),
            pl.BlockSpec((None, S, kv_w), lambda b, i: (b, 0, 0)),
            pl.BlockSpec((None, kv_w, S), lambda b, i: (b, 0, 0)),
        ],
        out_specs=pl.BlockSpec((None, tq_, q_w), lambda b, i: (b, i, 0)),
        out_shape=jax.ShapeDtypeStruct((B, S, q_w), BF16),
        compiler_params=params(dimension_semantics=("arbitrary", "arbitrary")),
        name="attn",
    )(qt, k, vt)

    tm = _tile(S, tm_post)
    hb = tm // HALO
    n_halo_blocks = S // HALO
    ple = p_all.shape[-1]
    d_ff = w["w_gate"].shape[1]
    group_w = pool_w_total // N_POOL
    y = pl.pallas_call(
        functools.partial(_post_kernel, seq_len=S),
        grid=(B, S // tm),
        in_specs=[
            pl.BlockSpec((None, tm, D), lambda b, i: (b, i, 0)),
            pl.BlockSpec((None, tm, pool_w_total), lambda b, i: (b, i, 0)),
            pl.BlockSpec((None, HALO, pool_w_total), lambda b, i: (b, jnp.maximum(i * hb - 1, 0), 0)),
            pl.BlockSpec((None, HALO, pool_w_total),
                         lambda b, i: (b, jnp.minimum((i + 1) * hb, n_halo_blocks - 1), 0)),
            pl.BlockSpec((None, tm, q_w), lambda b, i: (b, i, 0)),
            pl.BlockSpec((None, None, tm, ple), lambda b, i: (layer, b, i, 0)),
            _const_spec((N_POOL, group_w, group_w)),
            _const_spec((1, pool_w_total)),
            _const_spec((pool_w_total + q_w, D)),
            _const_spec((1, D)),
            _const_spec((1, D)),
            _const_spec((1, D)),
            _const_spec((D, d_ff)),
            _const_spec((D, d_ff)),
            _const_spec((d_ff, D)),
            _const_spec((ple, D)),
            _const_spec((D, D)),
            _const_spec((1, D)),
        ],
        out_specs=pl.BlockSpec((None, tm, D), lambda b, i: (b, i, 0)),
        out_shape=jax.ShapeDtypeStruct((B, S, D), F32),
        scratch_shapes=[pltpu.VMEM((tm + 2 * HALO, pool_w_total), F32)],
        compiler_params=params(dimension_semantics=("arbitrary", "arbitrary")),
        name="post",
    )(x, u, u, u, attn, p_all, w["pool_w"], w["pool_scale"], w["w_out"], w["g_mix_post"],
      w["g_ffn_pre"], w["g_ffn_post"], w["w_gate"], w["w_up"], w["w_down"],
      w["w_ple_proj"], w["w_ple_gate"], w["b_ple_gate"])
    return y


def _prep_layer_weights(i, ln_mix_pre, ln_mix_post, w_in, pool_w, pool_scale, q_norm, k_norm, w_out,
                        ln_ffn_pre, ln_ffn_post, w_gate, w_up, w_down, w_ple_proj, w_ple_gate,
                        b_ple_gate):
    pool_w_total = pool_scale.shape[-1]
    kv_w = N_KV * HEAD_DIM
    in_w = w_in.shape[-1]
    q_w = in_w - pool_w_total - 2 * kv_w
    perm = np.concatenate([np.arange(0, HEAD_DIM, 2), np.arange(1, HEAD_DIM, 2)])
    n_rot_heads = (q_w + kv_w) // HEAD_DIM
    cols = np.concatenate([pool_w_total + hd * HEAD_DIM + perm for hd in range(n_rot_heads)]
                          + [np.arange(pool_w_total + q_w + kv_w, in_w)])
    wi = w_in[i]
    row = lambda v: v[i].reshape(1, -1).astype(F32)
    return {
        "g_mix_pre": row(ln_mix_pre),
        "wu": wi[:, :pool_w_total].astype(BF16),
        "wqkvt": wi[:, cols].T.astype(BF16),
        "qg": q_norm[i][perm].reshape(HEAD_DIM, 1).astype(F32),
        "kg": k_norm[i][perm].reshape(HEAD_DIM, 1).astype(F32),
        "pool_w": pool_w[i].astype(BF16),
        "pool_scale": row(pool_scale),
        "w_out": w_out[i].astype(BF16),
        "g_mix_post": row(ln_mix_post),
        "g_ffn_pre": row(ln_ffn_pre),
        "g_ffn_post": row(ln_ffn_post),
        "w_gate": w_gate[i].astype(BF16),
        "w_up": w_up[i].astype(BF16),
        "w_down": w_down[i].astype(BF16),
        "w_ple_proj": w_ple_proj[i].astype(BF16),
        "w_ple_gate": w_ple_gate[i].astype(BF16),
        "b_ple_gate": row(b_ple_gate),
    }


def kernel(x_prompt, x_sample, p_prompt, p_sample, ln_mix_pre, ln_mix_post, w_in, pool_w, pool_scale,
           q_norm, k_norm, w_out, ln_ffn_pre, ln_ffn_post, w_gate, w_up, w_down,
           w_ple_proj, w_ple_gate, b_ple_gate):
    depth = w_in.shape[0]
    layers = [
        _prep_layer_weights(i, ln_mix_pre, ln_mix_post, w_in, pool_w, pool_scale, q_norm, k_norm, w_out,
                            ln_ffn_pre, ln_ffn_post, w_gate, w_up, w_down, w_ple_proj, w_ple_gate,
                            b_ple_gate)
        for i in range(depth)
    ]

    def trunk(x, p):
        for i in range(depth):
            x = _layer(x, p, i, layers[i])
        return x

    return (trunk(x_prompt, p_prompt), trunk(x_sample, p_sample))
```

```python
import functools
import math

import jax
import jax.numpy as jnp
import numpy as np
from jax import lax
from jax.experimental import pallas as pl
from jax.experimental.pallas import tpu as pltpu

F32 = jnp.float32
BF16 = jnp.bfloat16

POOL_WINDOWS = (2, 4, 8, 16)
N_POOL = len(POOL_WINDOWS)
HALO = 8
HEAD_DIM = 64
HALF = HEAD_DIM // 2
N_KV = 2
GRID_W = 64
ROPE_THETA = 10000.0
EPS = 1e-6
Q_SCALE = (1.0 / math.sqrt(HEAD_DIM)) * math.log2(math.e)

V7X_VMEM_LIMIT_BYTES = 56 * 1024 * 1024


def _rms(x, g):
    ms = jnp.mean(x * x, axis=-1, keepdims=True)
    return x * lax.rsqrt(ms + EPS) * g


def _pre_kernel(x_ref, g_ref, wu_ref, wqkvt_ref, qg_ref, kg_ref, cos_ref, sin_ref,
                u_ref, qt_ref, k_ref, vt_ref, *, n_heads):
    h = _rms(x_ref[...], g_ref[...]).astype(BF16)
    u_ref[...] = jnp.dot(h, wu_ref[...], preferred_element_type=F32)
    zt = lax.dot_general(wqkvt_ref[...], h, (((1,), (1,)), ((), ())),
                         preferred_element_type=F32)
    cos = cos_ref[...]
    sin = sin_ref[...]

    def norm_rope(blk, g):
        ssq = jnp.sum(blk * blk, axis=0, keepdims=True)
        n = blk * lax.rsqrt(ssq * (1.0 / HEAD_DIM) + EPS) * g
        x0 = n[:HALF]
        x1 = n[HALF:]
        return jnp.concatenate([x0 * cos - x1 * sin, x0 * sin + x1 * cos], axis=0)

    qg = qg_ref[...]
    kg = kg_ref[...]
    for hd in range(n_heads):
        r = hd * HEAD_DIM
        qt_ref[r:r + HEAD_DIM, :] = (norm_rope(zt[r:r + HEAD_DIM], qg) * Q_SCALE).astype(BF16)
    q_w = n_heads * HEAD_DIM
    kt = jnp.concatenate(
        [norm_rope(zt[q_w + j * HEAD_DIM:q_w + (j + 1) * HEAD_DIM], kg) for j in range(N_KV)],
        axis=0)
    k_ref[...] = kt.T.astype(BF16)
    kv_w = N_KV * HEAD_DIM
    vt_ref[...] = zt[q_w + kv_w:q_w + 2 * kv_w].astype(BF16)


def _attn_kernel(qt_ref, k_ref, vt_ref, o_ref, s_ref, *, n_heads, kc):
    seq = k_ref.shape[0]
    tq = qt_ref.shape[1]
    n_chunks = seq // kc
    group = n_heads // N_KV
    sub = 8

    def fold(x, op):
        return op(x.reshape(kc // sub, sub, tq), axis=0)

    outs = [None] * n_heads
    mb = None
    for stage in range(n_heads + 1):
        hq, hs = stage, stage - 1
        if hq < n_heads:
            qh = qt_ref[hq * HEAD_DIM:(hq + 1) * HEAD_DIM, :]
            parts = [jnp.zeros_like(qh)] * N_KV
            parts[hq // group] = qh
            wq = jnp.concatenate(parts, axis=0)
            cmax = None
        if hs >= 0:
            j = hs // group
            mb_cur = mb
            csum = jnp.zeros((sub, tq), F32)
            acc = jnp.zeros((HEAD_DIM, tq), F32)
        for c in range(n_chunks):
            rows = slice(c * kc, (c + 1) * kc)
            if hq < n_heads:
                s = jnp.dot(k_ref[rows, :], wq, preferred_element_type=F32)
                s_ref[hq % 2, rows, :] = s
                cm = fold(s, jnp.max)
                cmax = cm if cmax is None else jnp.maximum(cmax, cm)
            if hs >= 0:
                s3 = s_ref[hs % 2, rows, :].reshape(kc // sub, sub, tq)
                p3 = jnp.exp2(s3 - mb_cur[None])
                csum = csum + jnp.sum(p3, axis=0)
                acc = acc + jnp.dot(vt_ref[j * HEAD_DIM:(j + 1) * HEAD_DIM, rows],
                                    p3.reshape(kc, tq).astype(BF16),
                                    preferred_element_type=F32)
        if hs >= 0:
            outs[hs] = acc / jnp.sum(csum, axis=0, keepdims=True)
        if hq < n_heads:
            mb = jnp.broadcast_to(jnp.max(cmax, axis=0, keepdims=True), (sub, tq))
    o_ref[...] = jnp.concatenate(outs, axis=0).T.astype(BF16)


def _post_kernel(x_ref, u_ref, uprev_ref, unext_ref, attn_ref, p_ref,
                 poolw_ref, pscale_ref, wout_ref, gmix_ref, gpre_ref, gpost_ref,
                 wgate_ref, wup_ref, wdown_ref, wproj_ref, wpg_ref, bpg_ref,
                 y_ref, uext_ref, *, seq_len):
    i = pl.program_id(1)
    tm = x_ref.shape[0]
    group_w = poolw_ref.shape[1]

    uext_ref[0:HALO, :] = jnp.where(i > 0, uprev_ref[...], 0.0)
    uext_ref[HALO:HALO + tm, :] = u_ref[...]
    uext_ref[HALO + tm:HALO + tm + HALO, :] = jnp.where(i < pl.num_programs(1) - 1, unext_ref[...], 0.0)

    t = lax.broadcasted_iota(jnp.int32, (tm, group_w), 0) + i * tm
    pooled = []
    for g, w in enumerate(POOL_WINDOWS):
        c0 = g * group_w
        back = w // 2
        fwd = w - 1 - back
        win = uext_ref[HALO - back:HALO - back + tm, c0:c0 + group_w]
        for d in range(-back + 1, fwd + 1):
            win = win + uext_ref[HALO + d:HALO + d + tm, c0:c0 + group_w]
        lo = jnp.maximum(t - back, 0)
        hi = jnp.minimum(t + fwd, seq_len - 1)
        cnt = (hi - lo + 1).astype(F32)
        m = win / cnt - uext_ref[HALO:HALO + tm, c0:c0 + group_w]
        pooled.append(jnp.dot(m.astype(BF16), poolw_ref[g], preferred_element_type=F32))
    pool_out = jnp.concatenate(pooled, axis=-1) * pscale_ref[...]

    cat = jnp.concatenate([pool_out.astype(BF16), attn_ref[...]], axis=-1)
    mix = jnp.dot(cat, wout_ref[...], preferred_element_type=F32)
    x1 = x_ref[...] + _rms(mix, gmix_ref[...])

    h = _rms(x1, gpre_ref[...]).astype(BF16)
    gate = jnp.dot(h, wgate_ref[...], preferred_element_type=F32)
    up = jnp.dot(h, wup_ref[...], preferred_element_type=F32)
    act = (gate * jax.nn.sigmoid(gate) * up).astype(BF16)
    f = jnp.dot(act, wdown_ref[...], preferred_element_type=F32)
    x2 = x1 + _rms(f, gpost_ref[...])

    e = jnp.dot(p_ref[...].astype(BF16), wproj_ref[...], preferred_element_type=F32)
    gt = jax.nn.sigmoid(jnp.dot(x2.astype(BF16), wpg_ref[...], preferred_element_type=F32)
                        + bpg_ref[...])
    y_ref[...] = x2 + e * gt


def _rope_tables_t(seq):
    rows = seq // GRID_W
    row = jnp.repeat(jnp.arange(rows, dtype=F32), GRID_W)
    col = jnp.tile(jnp.arange(GRID_W, dtype=F32), rows)
    inv = 1.0 / (ROPE_THETA ** (jnp.arange(0, HALF, 2, dtype=F32) / HALF))
    ang = jnp.concatenate([row[:, None] * inv, col[:, None] * inv], axis=-1)
    return jnp.cos(ang).T, jnp.sin(ang).T


def _const_spec(shape):
    nd = len(shape)
    return pl.BlockSpec(shape, lambda b, i: (0,) * nd, pipeline_mode=pl.Buffered(1))


def _tile(seq, want):
    tm = min(want, seq)
    assert seq % tm == 0 and tm % 128 == 0
    return tm


def _layer(x, p_all, layer, w, *, tm_pre=512, tq=256, kc=2048, tm_post=256):
    B, S, D = x.shape
    pool_w_total = w["wu"].shape[1]
    qkv_w = w["wqkvt"].shape[0]
    kv_w = N_KV * HEAD_DIM
    q_w = qkv_w - 2 * kv_w
    n_heads = q_w // HEAD_DIM
    assert S % GRID_W == 0 and S % HALO == 0
    cos_t, sin_t = _rope_tables_t(S)
    params = functools.partial(pltpu.CompilerParams, vmem_limit_bytes=V7X_VMEM_LIMIT_BYTES)

    tm = _tile(S, tm_pre)
    u, qt, k, vt = pl.pallas_call(
        functools.partial(_pre_kernel, n_heads=n_heads),
        grid=(B, S // tm),
        in_specs=[
            pl.BlockSpec((None, tm, D), lambda b, i: (b, i, 0)),
            _const_spec((1, D)),
            _const_spec((D, pool_w_total)),
            _const_spec((qkv_w, D)),
            _const_spec((HEAD_DIM, 1)),
            _const_spec((HEAD_DIM, 1)),
            pl.BlockSpec((HALF, tm), lambda b, i: (0, i)),
            pl.BlockSpec((HALF, tm), lambda b, i: (0, i)),
        ],
        out_specs=[
            pl.BlockSpec((None, tm, pool_w_total), lambda b, i: (b, i, 0)),
            pl.BlockSpec((None, q_w, tm), lambda b, i: (b, 0, i)),
            pl.BlockSpec((None, tm, kv_w), lambda b, i: (b, i, 0)),
            pl.BlockSpec((None, kv_w, tm), lambda b, i: (b, 0, i)),
        ],
        out_shape=[
            jax.ShapeDtypeStruct((B, S, pool_w_total), F32),
            jax.ShapeDtypeStruct((B, q_w, S), BF16),
            jax.ShapeDtypeStruct((B, S, kv_w), BF16),
            jax.ShapeDtypeStruct((B, kv_w, S), BF16),
        ],
        compiler_params=params(dimension_semantics=("arbitrary", "arbitrary")),
        name="pre",
    )(x, w["g_mix_pre"], w["wu"], w["wqkvt"], w["qg"], w["kg"], cos_t, sin_t)

    tq_ = _tile(S, tq)
    kc_ = _tile(S, kc)
    attn = pl.pallas_call(
        functools.partial(_attn_kernel, n_heads=n_heads, kc=kc_),
        grid=(B, S // tq_),
        in_specs=[
            pl.BlockSpec((None, q_w, tq_), lambda b, i: (b, 0, i)),
            pl.BlockSpec((None, S, kv_w), lambda b, i: (b, 0, 0)),
            pl.BlockSpec((None, kv_w, S), lambda b, i: (b, 0, 0)),
        ],
        out_specs=pl.BlockSpec((None, tq_, q_w), lambda b, i: (b, i, 0)),
        out_shape=jax.ShapeDtypeStruct((B, S, q_w), BF16),
        scratch_shapes=[pltpu.VMEM((2, S, tq_), F32)],
        compiler_params=params(dimension_semantics=("arbitrary", "arbitrary")),
        name="attn",
    )(qt, k, vt)

    tm = _tile(S, tm_post)
    hb = tm // HALO
    n_halo_blocks = S // HALO
    ple = p_all.shape[-1]
    d_ff = w["w_gate"].shape[1]
    group_w = pool_w_total // N_POOL
    y = pl.pallas_call(
        functools.partial(_post_kernel, seq_len=S),
        grid=(B, S // tm),
        in_specs=[
            pl.BlockSpec((None, tm, D), lambda b, i: (b, i, 0)),
            pl.BlockSpec((None, tm, pool_w_total), lambda b, i: (b, i, 0)),
            pl.BlockSpec((None, HALO, pool_w_total), lambda b, i: (b, jnp.maximum(i * hb - 1, 0), 0)),
            pl.BlockSpec((None, HALO, pool_w_total),
                         lambda b, i: (b, jnp.minimum((i + 1) * hb, n_halo_blocks - 1), 0)),
            pl.BlockSpec((None, tm, q_w), lambda b, i: (b, i, 0)),
            pl.BlockSpec((None, None, tm, ple), lambda b, i: (layer, b, i, 0)),
            _const_spec((N_POOL, group_w, group_w)),
            _const_spec((1, pool_w_total)),
            _const_spec((pool_w_total + q_w, D)),
            _const_spec((1, D)),
            _const_spec((1, D)),
            _const_spec((1, D)),
            _const_spec((D, d_ff)),
            _const_spec((D, d_ff)),
            _const_spec((d_ff, D)),
            _const_spec((ple, D)),
            _const_spec((D, D)),
            _const_spec((1, D)),
        ],
        out_specs=pl.BlockSpec((None, tm, D), lambda b, i: (b, i, 0)),
        out_shape=jax.ShapeDtypeStruct((B, S, D), F32),
        scratch_shapes=[pltpu.VMEM((tm + 2 * HALO, pool_w_total), F32)],
        compiler_params=params(dimension_semantics=("arbitrary", "arbitrary")),
        name="post",
    )(x, u, u, u, attn, p_all, w["pool_w"], w["pool_scale"], w["w_out"], w["g_mix_post"],
      w["g_ffn_pre"], w["g_ffn_post"], w["w_gate"], w["w_up"], w["w_down"],
      w["w_ple_proj"], w["w_ple_gate"], w["b_ple_gate"])
    return y


def _prep_layer_weights(i, ln_mix_pre, ln_mix_post, w_in, pool_w, pool_scale, q_norm, k_norm, w_out,
                        ln_ffn_pre, ln_ffn_post, w_gate, w_up, w_down, w_ple_proj, w_ple_gate,
                        b_ple_gate):
    pool_w_total = pool_scale.shape[-1]
    kv_w = N_KV * HEAD_DIM
    in_w = w_in.shape[-1]
    q_w = in_w - pool_w_total - 2 * kv_w
    perm = np.concatenate([np.arange(0, HEAD_DIM, 2), np.arange(1, HEAD_DIM, 2)])
    n_rot_heads = (q_w + kv_w) // HEAD_DIM
    cols = np.concatenate([pool_w_total + hd * HEAD_DIM + perm for hd in range(n_rot_heads)]
                          + [np.arange(pool_w_total + q_w + kv_w, in_w)])
    wi = w_in[i]
    row = lambda v: v[i].reshape(1, -1).astype(F32)
    return {
        "g_mix_pre": row(ln_mix_pre),
        "wu": wi[:, :pool_w_total].astype(BF16),
        "wqkvt": wi[:, cols].T.astype(BF16),
        "qg": q_norm[i][perm].reshape(HEAD_DIM, 1).astype(F32),
        "kg": k_norm[i][perm].reshape(HEAD_DIM, 1).astype(F32),
        "pool_w": pool_w[i].astype(BF16),
        "pool_scale": row(pool_scale),
        "w_out": w_out[i].astype(BF16),
        "g_mix_post": row(ln_mix_post),
        "g_ffn_pre": row(ln_ffn_pre),
        "g_ffn_post": row(ln_ffn_post),
        "w_gate": w_gate[i].astype(BF16),
        "w_up": w_up[i].astype(BF16),
        "w_down": w_down[i].astype(BF16),
        "w_ple_proj": w_ple_proj[i].astype(BF16),
        "w_ple_gate": w_ple_gate[i].astype(BF16),
        "b_ple_gate": row(b_ple_gate),
    }


def kernel(x_prompt, x_sample, p_prompt, p_sample, ln_mix_pre, ln_mix_post, w_in, pool_w, pool_scale,
           q_norm, k_norm, w_out, ln_ffn_pre, ln_ffn_post, w_gate, w_up, w_down,
           w_ple_proj, w_ple_gate, b_ple_gate):
    depth = w_in.shape[0]
    layers = [
        _prep_layer_weights(i, ln_mix_pre, ln_mix_post, w_in, pool_w, pool_scale, q_norm, k_norm, w_out,
                            ln_ffn_pre, ln_ffn_post, w_gate, w_up, w_down, w_ple_proj, w_ple_gate,
                            b_ple_gate)
        for i in range(depth)
    ]

    def trunk(x, p):
        for i in range(depth):
            x = _layer(x, p, i, layers[i])
        return x

    return (trunk(x_prompt, p_prompt), trunk(x_sample, p_sample))
```

```python
import functools
import math

import jax
import jax.numpy as jnp
import numpy as np
from jax import lax
from jax.experimental import pallas as pl
from jax.experimental.pallas import tpu as pltpu

F32 = jnp.float32
BF16 = jnp.bfloat16

POOL_WINDOWS = (2, 4, 8, 16)
N_POOL = len(POOL_WINDOWS)
HALO = 8
assert all(w & (w - 1) == 0 and 2 <= w <= 2 * HALO for w in POOL_WINDOWS)
HEAD_DIM = 64
HALF = HEAD_DIM // 2
N_KV = 2
GRID_W = 64
ROPE_THETA = 10000.0
EPS = 1e-6
Q_SCALE = (1.0 / math.sqrt(HEAD_DIM)) * math.log2(math.e)

V7X_VMEM_LIMIT_BYTES = 56 * 1024 * 1024


def _rms(x, g):
    ms = jnp.mean(x * x, axis=-1, keepdims=True)
    return x * lax.rsqrt(ms + EPS) * g


def _pre_kernel(x_ref, g_ref, wu_ref, wqkvt_ref, qg_ref, kg_ref, cos_ref, sin_ref,
                u_ref, qt_ref, k_ref, vt_ref, *, n_heads, n_sub):
    tm = x_ref.shape[0]
    hm = tm // n_sub
    q_w = n_heads * HEAD_DIM
    kv_w = N_KV * HEAD_DIM
    qg = qg_ref[...]
    kg = kg_ref[...]

    def phases(r0):
        rows = slice(r0, r0 + hm)
        h = _rms(x_ref[rows, :], g_ref[...]).astype(BF16)
        yield
        u_ref[rows, :] = jnp.dot(h, wu_ref[...], preferred_element_type=F32)
        zt = lax.dot_general(wqkvt_ref[...], h, (((1,), (1,)), ((), ())),
                             preferred_element_type=F32)
        yield
        cos = cos_ref[:, rows]
        sin = sin_ref[:, rows]

        def norm_rope(blk, g):
            ssq = jnp.sum(blk * blk, axis=0, keepdims=True)
            n = blk * lax.rsqrt(ssq * (1.0 / HEAD_DIM) + EPS) * g
            x0 = n[:HALF]
            x1 = n[HALF:]
            return jnp.concatenate([x0 * cos - x1 * sin, x0 * sin + x1 * cos], axis=0)

        for hd in range(n_heads):
            r = hd * HEAD_DIM
            qt_ref[r:r + HEAD_DIM, rows] = (norm_rope(zt[r:r + HEAD_DIM], qg) * Q_SCALE).astype(BF16)
        kt = jnp.concatenate(
            [norm_rope(zt[q_w + j * HEAD_DIM:q_w + (j + 1) * HEAD_DIM], kg) for j in range(N_KV)],
            axis=0)
        k_ref[rows, :] = kt.T.astype(BF16)
        vt_ref[:, rows] = zt[q_w + kv_w:q_w + 2 * kv_w].astype(BF16)
        yield

    for _ in zip(*[phases(sidx * hm) for sidx in range(n_sub)]):
        pass


def _attn_kernel(qt_ref, k_ref, vt_ref, o_ref, s_ref, *, n_heads, kc):
    seq = k_ref.shape[0]
    tq = qt_ref.shape[1]
    n_chunks = seq // kc
    group = n_heads // N_KV
    sub = 8

    def fold(x, op):
        return op(x.reshape(kc // sub, sub, tq), axis=0)

    outs = [None] * n_heads
    mb = None
    for stage in range(n_heads + 1):
        hq, hs = stage, stage - 1
        if hq < n_heads:
            qh = qt_ref[hq * HEAD_DIM:(hq + 1) * HEAD_DIM, :]
            parts = [jnp.zeros_like(qh)] * N_KV
            parts[hq // group] = qh
            wq = jnp.concatenate(parts, axis=0)
            cmax = None
        if hs >= 0:
            j = hs // group
            mb_cur = mb
            csum = jnp.zeros((sub, tq), F32)
            acc = jnp.zeros((HEAD_DIM, tq), F32)
        for c in range(n_chunks):
            rows = slice(c * kc, (c + 1) * kc)
            if hq < n_heads:
                s = jnp.dot(k_ref[rows, :], wq, preferred_element_type=F32)
                s_ref[hq % 2, rows, :] = s
                cm = fold(s, jnp.max)
                cmax = cm if cmax is None else jnp.maximum(cmax, cm)
            if hs >= 0:
                s3 = s_ref[hs % 2, rows, :].reshape(kc // sub, sub, tq)
                p3 = jnp.exp2(s3 - mb_cur[None])
                csum = csum + jnp.sum(p3, axis=0)
                acc = acc + jnp.dot(vt_ref[j * HEAD_DIM:(j + 1) * HEAD_DIM, rows],
                                    p3.reshape(kc, tq).astype(BF16),
                                    preferred_element_type=F32)
        if hs >= 0:
            outs[hs] = acc / jnp.sum(csum, axis=0, keepdims=True)
        if hq < n_heads:
            mb = jnp.broadcast_to(jnp.max(cmax, axis=0, keepdims=True), (sub, tq))
    o_ref[...] = jnp.concatenate(outs, axis=0).T.astype(BF16)


def _post_kernel(x_ref, u_ref, uprev_ref, unext_ref, attn_ref, p_ref,
                 poolw_ref, pscale_ref, wout_ref, gmix_ref, gpre_ref, gpost_ref,
                 wgate_ref, wup_ref, wdown_ref, wproj_ref, wpg_ref, bpg_ref,
                 y_ref, uext_ref, *, seq_len, n_sub):
    i = pl.program_id(1)
    tm = x_ref.shape[0]
    hm = tm // n_sub
    group_w = poolw_ref.shape[1]

    uext_ref[0:HALO, :] = jnp.where(i > 0, uprev_ref[...], 0.0)
    uext_ref[HALO:HALO + tm, :] = u_ref[...]
    uext_ref[HALO + tm:HALO + tm + HALO, :] = jnp.where(i < pl.num_programs(1) - 1, unext_ref[...], 0.0)

    def phases(r0):
        rows = slice(r0, r0 + hm)
        ext = hm + 2 * HALO
        t_first = lax.broadcasted_iota(jnp.int32, (HALO, group_w), 0) + (i * tm + r0)
        pooled = []
        for g, w in enumerate(POOL_WINDOWS):
            c0 = g * group_w
            back = w // 2
            fwd = w - 1 - back
            xe = uext_ref[r0:r0 + ext, c0:c0 + group_w]
            a = xe
            step = 1
            while step < back:
                a = a + pltpu.roll(a, ext - step, axis=0)
                step *= 2
            win = (a + pltpu.roll(a, back, axis=0))[HALO:HALO + hm]

            def inv_cnt(t8):
                lo = jnp.maximum(t8 - back, 0)
                hi = jnp.minimum(t8 + fwd, seq_len - 1)
                return 1.0 / (hi - lo + 1).astype(F32)

            inv = jnp.concatenate([inv_cnt(t_first),
                                   jnp.full((hm - 2 * HALO, group_w), 1.0 / w, F32),
                                   inv_cnt(t_first + (hm - HALO))], axis=0)
            m = win * inv - xe[HALO:HALO + hm]
            pooled.append(jnp.dot(m.astype(BF16), poolw_ref[g], preferred_element_type=F32))
        pool_out = jnp.concatenate(pooled, axis=-1) * pscale_ref[...]
        yield
        cat = jnp.concatenate([pool_out.astype(BF16), attn_ref[rows, :]], axis=-1)
        mix = jnp.dot(cat, wout_ref[...], preferred_element_type=F32)
        yield
        x1 = x_ref[rows, :] + _rms(mix, gmix_ref[...])
        h = _rms(x1, gpre_ref[...]).astype(BF16)
        yield
        gate = jnp.dot(h, wgate_ref[...], preferred_element_type=F32)
        up = jnp.dot(h, wup_ref[...], preferred_element_type=F32)
        yield
        act = (gate * jax.nn.sigmoid(gate) * up).astype(BF16)
        yield
        f = jnp.dot(act, wdown_ref[...], preferred_element_type=F32)
        yield
        x2 = x1 + _rms(f, gpost_ref[...])
        xb = x2.astype(BF16)
        yield
        e = jnp.dot(p_ref[rows, :].astype(BF16), wproj_ref[...], preferred_element_type=F32)
        gt = jax.nn.sigmoid(jnp.dot(xb, wpg_ref[...], preferred_element_type=F32) + bpg_ref[...])
        y_ref[rows, :] = x2 + e * gt
        yield

    for _ in zip(*[phases(sidx * hm) for sidx in range(n_sub)]):
        pass


def _rope_tables_t(seq):
    rows = seq // GRID_W
    row = jnp.repeat(jnp.arange(rows, dtype=F32), GRID_W)
    col = jnp.tile(jnp.arange(GRID_W, dtype=F32), rows)
    inv = 1.0 / (ROPE_THETA ** (jnp.arange(0, HALF, 2, dtype=F32) / HALF))
    ang = jnp.concatenate([row[:, None] * inv, col[:, None] * inv], axis=-1)
    return jnp.cos(ang).T, jnp.sin(ang).T


def _const_spec(shape):
    nd = len(shape)
    return pl.BlockSpec(shape, lambda b, i: (0,) * nd, pipeline_mode=pl.Buffered(1))


def _tile(seq, want):
    tm = min(want, seq)
    assert seq % tm == 0 and tm % 128 == 0
    return tm


def _layer(x, p_all, layer, w, *, tm_pre=512, pre_sub=2, tq=256, kc=2048, tm_post=512, post_sub=2):
    B, S, D = x.shape
    pool_w_total = w["wu"].shape[1]
    qkv_w = w["wqkvt"].shape[0]
    kv_w = N_KV * HEAD_DIM
    q_w = qkv_w - 2 * kv_w
    n_heads = q_w // HEAD_DIM
    assert S % GRID_W == 0 and S % HALO == 0
    cos_t, sin_t = _rope_tables_t(S)
    params = functools.partial(pltpu.CompilerParams, vmem_limit_bytes=V7X_VMEM_LIMIT_BYTES)

    tm = _tile(S, tm_pre)
    u, qt, k, vt = pl.pallas_call(
        functools.partial(_pre_kernel, n_heads=n_heads, n_sub=pre_sub),
        grid=(B, S // tm),
        in_specs=[
            pl.BlockSpec((None, tm, D), lambda b, i: (b, i, 0)),
            _const_spec((1, D)),
            _const_spec((D, pool_w_total)),
            _const_spec((qkv_w, D)),
            _const_spec((HEAD_DIM, 1)),
            _const_spec((HEAD_DIM, 1)),
            pl.BlockSpec((HALF, tm), lambda b, i: (0, i)),
            pl.BlockSpec((HALF, tm), lambda b, i: (0, i)),
        ],
        out_specs=[
            pl.BlockSpec((None, tm, pool_w_total), lambda b, i: (b, i, 0)),
            pl.BlockSpec((None, q_w, tm), lambda b, i: (b, 0, i)),
            pl.BlockSpec((None, tm, kv_w), lambda b, i: (b, i, 0)),
            pl.BlockSpec((None, kv_w, tm), lambda b, i: (b, 0, i)),
        ],
        out_shape=[
            jax.ShapeDtypeStruct((B, S, pool_w_total), F32),
            jax.ShapeDtypeStruct((B, q_w, S), BF16),
            jax.ShapeDtypeStruct((B, S, kv_w), BF16),
            jax.ShapeDtypeStruct((B, kv_w, S), BF16),
        ],
        compiler_params=params(dimension_semantics=("arbitrary", "arbitrary")),
        name="pre",
    )(x, w["g_mix_pre"], w["wu"], w["wqkvt"], w["qg"], w["kg"], cos_t, sin_t)

    tq_ = _tile(S, tq)
    kc_ = _tile(S, kc)
    attn = pl.pallas_call(
        functools.partial(_attn_kernel, n_heads=n_heads, kc=kc_),
        grid=(B, S // tq_),
        in_specs=[
            pl.BlockSpec((None, q_w, tq_), lambda b, i: (b, 0, i)),
            pl.BlockSpec((None, S, kv_w), lambda b, i: (b, 0, 0)),
            pl.BlockSpec((None, kv_w, S), lambda b, i: (b, 0, 0)),
        ],
        out_specs=pl.BlockSpec((None, tq_, q_w), lambda b, i: (b, i, 0)),
        out_shape=jax.ShapeDtypeStruct((B, S, q_w), BF16),
        scratch_shapes=[pltpu.VMEM((2, S, tq_), F32)],
        compiler_params=params(dimension_semantics=("arbitrary", "arbitrary")),
        name="attn",
    )(qt, k, vt)

    tm = _tile(S, tm_post)
    assert tm % (post_sub * HALO) == 0
    hb = tm // HALO
    n_halo_blocks = S // HALO
    ple = p_all.shape[-1]
    d_ff = w["w_gate"].shape[1]
    group_w = pool_w_total // N_POOL
    y = pl.pallas_call(
        functools.partial(_post_kernel, seq_len=S, n_sub=post_sub),
        grid=(B, S // tm),
        in_specs=[
            pl.BlockSpec((None, tm, D), lambda b, i: (b, i, 0)),
            pl.BlockSpec((None, tm, pool_w_total), lambda b, i: (b, i, 0)),
            pl.BlockSpec((None, HALO, pool_w_total), lambda b, i: (b, jnp.maximum(i * hb - 1, 0), 0)),
            pl.BlockSpec((None, HALO, pool_w_total),
                         lambda b, i: (b, jnp.minimum((i + 1) * hb, n_halo_blocks - 1), 0)),
            pl.BlockSpec((None, tm, q_w), lambda b, i: (b, i, 0)),
            pl.BlockSpec((None, None, tm, ple), lambda b, i: (layer, b, i, 0)),
            _const_spec((N_POOL, group_w, group_w)),
            _const_spec((1, pool_w_total)),
            _const_spec((pool_w_total + q_w, D)),
            _const_spec((1, D)),
            _const_spec((1, D)),
            _const_spec((1, D)),
            _const_spec((D, d_ff)),
            _const_spec((D, d_ff)),
            _const_spec((d_ff, D)),
            _const_spec((ple, D)),
            _const_spec((D, D)),
            _const_spec((1, D)),
        ],
        out_specs=pl.BlockSpec((None, tm, D), lambda b, i: (b, i, 0)),
        out_shape=jax.ShapeDtypeStruct((B, S, D), F32),
        scratch_shapes=[pltpu.VMEM((tm + 2 * HALO, pool_w_total), F32)],
        compiler_params=params(dimension_semantics=("arbitrary", "arbitrary")),
        name="post",
    )(x, u, u, u, attn, p_all, w["pool_w"], w["pool_scale"], w["w_out"], w["g_mix_post"],
      w["g_ffn_pre"], w["g_ffn_post"], w["w_gate"], w["w_up"], w["w_down"],
      w["w_ple_proj"], w["w_ple_gate"], w["b_ple_gate"])
    return y


def _prep_layer_weights(i, ln_mix_pre, ln_mix_post, w_in, pool_w, pool_scale, q_norm, k_norm, w_out,
                        ln_ffn_pre, ln_ffn_post, w_gate, w_up, w_down, w_ple_proj, w_ple_gate,
                        b_ple_gate):
    pool_w_total = pool_scale.shape[-1]
    kv_w = N_KV * HEAD_DIM
    in_w = w_in.shape[-1]
    q_w = in_w - pool_w_total - 2 * kv_w
    perm = np.concatenate([np.arange(0, HEAD_DIM, 2), np.arange(1, HEAD_DIM, 2)])
    n_rot_heads = (q_w + kv_w) // HEAD_DIM
    cols = np.concatenate([pool_w_total + hd * HEAD_DIM + perm for hd in range(n_rot_heads)]
                          + [np.arange(pool_w_total + q_w + kv_w, in_w)])
    wi = w_in[i]
    row = lambda v: v[i].reshape(1, -1).astype(F32)
    return {
        "g_mix_pre": row(ln_mix_pre),
        "wu": wi[:, :pool_w_total].astype(BF16),
        "wqkvt": wi[:, cols].T.astype(BF16),
        "qg": q_norm[i][perm].reshape(HEAD_DIM, 1).astype(F32),
        "kg": k_norm[i][perm].reshape(HEAD_DIM, 1).astype(F32),
        "pool_w": pool_w[i].astype(BF16),
        "pool_scale": row(pool_scale),
        "w_out": w_out[i].astype(BF16),
        "g_mix_post": row(ln_mix_post),
        "g_ffn_pre": row(ln_ffn_pre),
        "g_ffn_post": row(ln_ffn_post),
        "w_gate": w_gate[i].astype(BF16),
        "w_up": w_up[i].astype(BF16),
        "w_down": w_down[i].astype(BF16),
        "w_ple_proj": w_ple_proj[i].astype(BF16),
        "w_ple_gate": w_ple_gate[i].astype(BF16),
        "b_ple_gate": row(b_ple_gate),
    }


def kernel(x_prompt, x_sample, p_prompt, p_sample, ln_mix_pre, ln_mix_post, w_in, pool_w, pool_scale,
           q_norm, k_norm, w_out, ln_ffn_pre, ln_ffn_post, w_gate, w_up, w_down,
           w_ple_proj, w_ple_gate, b_ple_gate):
    depth = w_in.shape[0]
    layers = [
        _prep_layer_weights(i, ln_mix_pre, ln_mix_post, w_in, pool_w, pool_scale, q_norm, k_norm, w_out,
                            ln_ffn_pre, ln_ffn_post, w_gate, w_up, w_down, w_ple_proj, w_ple_gate,
                            b_ple_gate)
        for i in range(depth)
    ]

    def trunk(x, p):
        for i in range(depth):
            x = _layer(x, p, i, layers[i])
        return x

    return (trunk(x_prompt, p_prompt), trunk(x_sample, p_sample))
```

```python
import functools
import math

import jax
import jax.numpy as jnp
import numpy as np
from jax import lax
from jax.experimental import pallas as pl
from jax.experimental.pallas import tpu as pltpu

F32 = jnp.float32
BF16 = jnp.bfloat16

POOL_WINDOWS = (2, 4, 8, 16)
N_POOL = len(POOL_WINDOWS)
HALO = 8
assert all(w & (w - 1) == 0 and 2 <= w <= 2 * HALO for w in POOL_WINDOWS)
HEAD_DIM = 64
HALF = HEAD_DIM // 2
N_KV = 2
GRID_W = 64
ROPE_THETA = 10000.0
EPS = 1e-6
Q_SCALE = (1.0 / math.sqrt(HEAD_DIM)) * math.log2(math.e)

V7X_VMEM_LIMIT_BYTES = 56 * 1024 * 1024


def _rms(x, g):
    ms = jnp.mean(x * x, axis=-1, keepdims=True)
    return x * lax.rsqrt(ms + EPS) * g


def _pre_kernel(x_ref, g_ref, wu_ref, wqkvt_ref, qg_ref, kg_ref, cos_ref, sin_ref,
                u_ref, qt_ref, k_ref, vt_ref, *, n_heads, n_sub):
    tm = x_ref.shape[0]
    hm = tm // n_sub
    q_w = n_heads * HEAD_DIM
    kv_w = N_KV * HEAD_DIM
    qg = qg_ref[...]
    kg = kg_ref[...]

    def phases(r0):
        rows = slice(r0, r0 + hm)
        h = _rms(x_ref[rows, :], g_ref[...]).astype(BF16)
        yield
        u_ref[rows, :] = jnp.dot(h, wu_ref[...], preferred_element_type=F32)
        zt = lax.dot_general(wqkvt_ref[...], h, (((1,), (1,)), ((), ())),
                             preferred_element_type=F32)
        yield
        cos = cos_ref[:, rows]
        sin = sin_ref[:, rows]

        def norm_rope(blk, g):
            ssq = jnp.sum(blk * blk, axis=0, keepdims=True)
            n = blk * lax.rsqrt(ssq * (1.0 / HEAD_DIM) + EPS) * g
            x0 = n[:HALF]
            x1 = n[HALF:]
            return jnp.concatenate([x0 * cos - x1 * sin, x0 * sin + x1 * cos], axis=0)

        for hd in range(n_heads):
            r = hd * HEAD_DIM
            qt_ref[r:r + HEAD_DIM, rows] = (norm_rope(zt[r:r + HEAD_DIM], qg) * Q_SCALE).astype(BF16)
        kt = jnp.concatenate(
            [norm_rope(zt[q_w + j * HEAD_DIM:q_w + (j + 1) * HEAD_DIM], kg) for j in range(N_KV)],
            axis=0)
        k_ref[rows, :] = kt.T.astype(BF16)
        vt_ref[:, rows] = zt[q_w + kv_w:q_w + 2 * kv_w].astype(BF16)
        yield

    for _ in zip(*[phases(sidx * hm) for sidx in range(n_sub)]):
        pass


def _attn_kernel(qt_ref, k_ref, vt_ref, o_ref, s_ref, *, n_heads, kc, tq):
    seq = k_ref.shape[0]
    n_qsub = qt_ref.shape[1] // tq
    n_chunks = seq // kc
    group = n_heads // N_KV
    sub = 8
    items = [(qb, hd) for qb in range(n_qsub) for hd in range(n_heads)]

    def fold(x, op):
        return op(x.reshape(kc // sub, sub, tq), axis=0)

    outs = {}
    mb = None
    for stage in range(len(items) + 1):
        scoring = stage < len(items)
        reducing = stage >= 1
        if scoring:
            qb, hq = items[stage]
            qh = qt_ref[hq * HEAD_DIM:(hq + 1) * HEAD_DIM, qb * tq:(qb + 1) * tq]
            parts = [jnp.zeros_like(qh)] * N_KV
            parts[hq // group] = qh
            wq = jnp.concatenate(parts, axis=0)
            cmax = None
        if reducing:
            j = items[stage - 1][1] // group
            mb_cur = mb
            csum = jnp.zeros((sub, tq), F32)
            acc = jnp.zeros((HEAD_DIM, tq), F32)
        for c in range(n_chunks):
            rows = slice(c * kc, (c + 1) * kc)
            if scoring:
                s = jnp.dot(k_ref[rows, :], wq, preferred_element_type=F32)
                s_ref[stage % 2, rows, :] = s
                cm = fold(s, jnp.max)
                cmax = cm if cmax is None else jnp.maximum(cmax, cm)
            if reducing:
                s3 = s_ref[(stage - 1) % 2, rows, :].reshape(kc // sub, sub, tq)
                p3 = jnp.exp2(s3 - mb_cur[None])
                csum = csum + jnp.sum(p3, axis=0)
                acc = acc + jnp.dot(vt_ref[j * HEAD_DIM:(j + 1) * HEAD_DIM, rows],
                                    p3.reshape(kc, tq).astype(BF16),
                                    preferred_element_type=F32)
        if reducing:
            outs[items[stage - 1]] = acc / jnp.sum(csum, axis=0, keepdims=True)
        if scoring:
            mb = jnp.broadcast_to(jnp.max(cmax, axis=0, keepdims=True), (sub, tq))
    for qb in range(n_qsub):
        ot = jnp.concatenate([outs[(qb, hd)] for hd in range(n_heads)], axis=0)
        o_ref[qb * tq:(qb + 1) * tq, :] = ot.T.astype(BF16)


def _post_kernel(x_ref, u_ref, uprev_ref, unext_ref, attn_ref, p_ref,
                 poolw_ref, pscale_ref, wout_ref, gmix_ref, gpre_ref, gpost_ref,
                 wgate_ref, wup_ref, wdown_ref, wproj_ref, wpg_ref, bpg_ref,
                 y_ref, uext_ref, *, seq_len, n_sub):
    i = pl.program_id(1)
    tm = x_ref.shape[0]
    hm = tm // n_sub
    group_w = poolw_ref.shape[1]

    uext_ref[0:HALO, :] = jnp.where(i > 0, uprev_ref[...], 0.0)
    uext_ref[HALO:HALO + tm, :] = u_ref[...]
    uext_ref[HALO + tm:HALO + tm + HALO, :] = jnp.where(i < pl.num_programs(1) - 1, unext_ref[...], 0.0)

    def phases(r0):
        rows = slice(r0, r0 + hm)
        ext = hm + 2 * HALO
        t_first = lax.broadcasted_iota(jnp.int32, (HALO, group_w), 0) + (i * tm + r0)
        pooled = []
        for g, w in enumerate(POOL_WINDOWS):
            c0 = g * group_w
            back = w // 2
            fwd = w - 1 - back
            xe = uext_ref[r0:r0 + ext, c0:c0 + group_w]
            a = xe
            step = 1
            while step < back:
                a = a + pltpu.roll(a, ext - step, axis=0)
                step *= 2
            win = (a + pltpu.roll(a, back, axis=0))[HALO:HALO + hm]

            def inv_cnt(t8):
                lo = jnp.maximum(t8 - back, 0)
                hi = jnp.minimum(t8 + fwd, seq_len - 1)
                return 1.0 / (hi - lo + 1).astype(F32)

            inv = jnp.concatenate([inv_cnt(t_first),
                                   jnp.full((hm - 2 * HALO, group_w), 1.0 / w, F32),
                                   inv_cnt(t_first + (hm - HALO))], axis=0)
            m = win * inv - xe[HALO:HALO + hm]
            pooled.append(jnp.dot(m.astype(BF16), poolw_ref[g], preferred_element_type=F32))
        pool_out = jnp.concatenate(pooled, axis=-1) * pscale_ref[...]
        yield
        cat = jnp.concatenate([pool_out.astype(BF16), attn_ref[rows, :]], axis=-1)
        mix = jnp.dot(cat, wout_ref[...], preferred_element_type=F32)
        yield
        x1 = x_ref[rows, :] + _rms(mix, gmix_ref[...])
        h = _rms(x1, gpre_ref[...]).astype(BF16)
        yield
        gate = jnp.dot(h, wgate_ref[...], preferred_element_type=F32)
        up = jnp.dot(h, wup_ref[...], preferred_element_type=F32)
        yield
        act = (gate * jax.nn.sigmoid(gate) * up).astype(BF16)
        yield
        f = jnp.dot(act, wdown_ref[...], preferred_element_type=F32)
        yield
        x2 = x1 + _rms(f, gpost_ref[...])
        xb = x2.astype(BF16)
        yield
        e = jnp.dot(p_ref[rows, :].astype(BF16), wproj_ref[...], preferred_element_type=F32)
        gt = jax.nn.sigmoid(jnp.dot(xb, wpg_ref[...], preferred_element_type=F32) + bpg_ref[...])
        y_ref[rows, :] = x2 + e * gt
        yield

    for _ in zip(*[phases(sidx * hm) for sidx in range(n_sub)]):
        pass


def _rope_tables_t(seq):
    rows = seq // GRID_W
    row = jnp.repeat(jnp.arange(rows, dtype=F32), GRID_W)
    col = jnp.tile(jnp.arange(GRID_W, dtype=F32), rows)
    inv = 1.0 / (ROPE_THETA ** (jnp.arange(0, HALF, 2, dtype=F32) / HALF))
    ang = jnp.concatenate([row[:, None] * inv, col[:, None] * inv], axis=-1)
    return jnp.cos(ang).T, jnp.sin(ang).T


def _const_spec(shape):
    nd = len(shape)
    return pl.BlockSpec(shape, lambda b, i: (0,) * nd, pipeline_mode=pl.Buffered(1))


def _tile(seq, want):
    tm = min(want, seq)
    assert seq % tm == 0 and tm % 128 == 0
    return tm


def _layer(x, p_all, layer, w, *, tm_pre=1024, pre_sub=4, tq=256, q_sub=2, kc=2048, tm_post=512, post_sub=2):
    B, S, D = x.shape
    pool_w_total = w["wu"].shape[1]
    qkv_w = w["wqkvt"].shape[0]
    kv_w = N_KV * HEAD_DIM
    q_w = qkv_w - 2 * kv_w
    n_heads = q_w // HEAD_DIM
    assert S % GRID_W == 0 and S % HALO == 0
    cos_t, sin_t = _rope_tables_t(S)
    params = functools.partial(pltpu.CompilerParams, vmem_limit_bytes=V7X_VMEM_LIMIT_BYTES)

    tm = _tile(S, tm_pre)
    u, qt, k, vt = pl.pallas_call(
        functools.partial(_pre_kernel, n_heads=n_heads, n_sub=pre_sub),
        grid=(B, S // tm),
        in_specs=[
            pl.BlockSpec((None, tm, D), lambda b, i: (b, i, 0)),
            _const_spec((1, D)),
            _const_spec((D, pool_w_total)),
            _const_spec((qkv_w, D)),
            _const_spec((HEAD_DIM, 1)),
            _const_spec((HEAD_DIM, 1)),
            pl.BlockSpec((HALF, tm), lambda b, i: (0, i)),
            pl.BlockSpec((HALF, tm), lambda b, i: (0, i)),
        ],
        out_specs=[
            pl.BlockSpec((None, tm, pool_w_total), lambda b, i: (b, i, 0)),
            pl.BlockSpec((None, q_w, tm), lambda b, i: (b, 0, i)),
            pl.BlockSpec((None, tm, kv_w), lambda b, i: (b, i, 0)),
            pl.BlockSpec((None, kv_w, tm), lambda b, i: (b, 0, i)),
        ],
        out_shape=[
            jax.ShapeDtypeStruct((B, S, pool_w_total), F32),
            jax.ShapeDtypeStruct((B, q_w, S), BF16),
            jax.ShapeDtypeStruct((B, S, kv_w), BF16),
            jax.ShapeDtypeStruct((B, kv_w, S), BF16),
        ],
        compiler_params=params(dimension_semantics=("arbitrary", "arbitrary")),
        name="pre",
    )(x, w["g_mix_pre"], w["wu"], w["wqkvt"], w["qg"], w["kg"], cos_t, sin_t)

    tq_ = _tile(S, tq)
    tqb = _tile(S, tq_ * q_sub)
    kc_ = _tile(S, kc)
    attn = pl.pallas_call(
        functools.partial(_attn_kernel, n_heads=n_heads, kc=kc_, tq=tq_),
        grid=(B, S // tqb),
        in_specs=[
            pl.BlockSpec((None, q_w, tqb), lambda b, i: (b, 0, i)),
            pl.BlockSpec((None, S, kv_w), lambda b, i: (b, 0, 0)),
            pl.BlockSpec((None, kv_w, S), lambda b, i: (b, 0, 0)),
        ],
        out_specs=pl.BlockSpec((None, tqb, q_w), lambda b, i: (b, i, 0)),
        out_shape=jax.ShapeDtypeStruct((B, S, q_w), BF16),
        scratch_shapes=[pltpu.VMEM((2, S, tq_), F32)],
        compiler_params=params(dimension_semantics=("arbitrary", "arbitrary")),
        name="attn",
    )(qt, k, vt)

    tm = _tile(S, tm_post)
    assert tm % (post_sub * HALO) == 0
    hb = tm // HALO
    n_halo_blocks = S // HALO
    ple = p_all.shape[-1]
    d_ff = w["w_gate"].shape[1]
    group_w = pool_w_total // N_POOL
    y = pl.pallas_call(
        functools.partial(_post_kernel, seq_len=S, n_sub=post_sub),
        grid=(B, S // tm),
        in_specs=[
            pl.BlockSpec((None, tm, D), lambda b, i: (b, i, 0)),
            pl.BlockSpec((None, tm, pool_w_total), lambda b, i: (b, i, 0)),
            pl.BlockSpec((None, HALO, pool_w_total), lambda b, i: (b, jnp.maximum(i * hb - 1, 0), 0)),
            pl.BlockSpec((None, HALO, pool_w_total),
                         lambda b, i: (b, jnp.minimum((i + 1) * hb, n_halo_blocks - 1), 0)),
            pl.BlockSpec((None, tm, q_w), lambda b, i: (b, i, 0)),
            pl.BlockSpec((None, None, tm, ple), lambda b, i: (layer, b, i, 0)),
            _const_spec((N_POOL, group_w, group_w)),
            _const_spec((1, pool_w_total)),
            _const_spec((pool_w_total + q_w, D)),
            _const_spec((1, D)),
            _const_spec((1, D)),
            _const_spec((1, D)),
            _const_spec((D, d_ff)),
            _const_spec((D, d_ff)),
            _const_spec((d_ff, D)),
            _const_spec((ple, D)),
            _const_spec((D, D)),
            _const_spec((1, D)),
        ],
        out_specs=pl.BlockSpec((None, tm, D), lambda b, i: (b, i, 0)),
        out_shape=jax.ShapeDtypeStruct((B, S, D), F32),
        scratch_shapes=[pltpu.VMEM((tm + 2 * HALO, pool_w_total), F32)],
        compiler_params=params(dimension_semantics=("arbitrary", "arbitrary")),
        name="post",
    )(x, u, u, u, attn, p_all, w["pool_w"], w["pool_scale"], w["w_out"], w["g_mix_post"],
      w["g_ffn_pre"], w["g_ffn_post"], w["w_gate"], w["w_up"], w["w_down"],
      w["w_ple_proj"], w["w_ple_gate"], w["b_ple_gate"])
    return y


def _prep_layer_weights(i,ln_mix_pre, ln_mix_post, w_in, pool_w, pool_scale, q_norm, k_norm, w_out,
                        ln_ffn_pre, ln_ffn_post, w_gate, w_up, w_down, w_ple_proj, w_ple_gate,
                        b_ple_gate):
    pool_w_total = pool_scale.shape[-1]
    kv_w = N_KV * HEAD_DIM
    in_w = w_in.shape[-1]
    q_w = in_w - pool_w_total - 2 * kv_w
    perm = np.concatenate([np.arange(0, HEAD_DIM, 2), np.arange(1, HEAD_DIM, 2)])
    n_rot_heads = (q_w + kv_w) // HEAD_DIM
    cols = np.concatenate([pool_w_total + hd * HEAD_DIM + perm for hd in range(n_rot_heads)]
                          + [np.arange(pool_w_total + q_w + kv_w, in_w)])
    wi = w_in[i]
    row = lambda v: v[i].reshape(1, -1).astype(F32)
    return {
        "g_mix_pre": row(ln_mix_pre),
        "wu": wi[:, :pool_w_total].astype(BF16),
        "wqkvt": wi[:, cols].T.astype(BF16),
        "qg": q_norm[i][perm].reshape(HEAD_DIM, 1).astype(F32),
        "kg": k_norm[i][perm].reshape(HEAD_DIM, 1).astype(F32),
        "pool_w": pool_w[i].astype(BF16),
        "pool_scale": row(pool_scale),
        "w_out": w_out[i].astype(BF16),
        "g_mix_post": row(ln_mix_post),
        "g_ffn_pre": row(ln_ffn_pre),
        "g_ffn_post": row(ln_ffn_post),
        "w_gate": w_gate[i].astype(BF16),
        "w_up": w_up[i].astype(BF16),
        "w_down": w_down[i].astype(BF16),
        "w_ple_proj": w_ple_proj[i].astype(BF16),
        "w_ple_gate": w_ple_gate[i].astype(BF16),
        "b_ple_gate": row(b_ple_gate),
    }


def kernel(x_prompt, x_sample, p_prompt, p_sample, ln_mix_pre, ln_mix_post, w_in, pool_w, pool_scale,
           q_norm, k_norm, w_out, ln_ffn_pre, ln_ffn_post, w_gate, w_up, w_down,
           w_ple_proj, w_ple_gate, b_ple_gate):
    depth = w_in.shape[0]
    layers = [
        _prep_layer_weights(i, ln_mix_pre, ln_mix_post, w_in, pool_w, pool_scale, q_norm, k_norm, w_out,
                            ln_ffn_pre, ln_ffn_post, w_gate, w_up, w_down, w_ple_proj, w_ple_gate,
                            b_ple_gate)
        for i in range(depth)
    ]

    def trunk(x, p):
        for i in range(depth):
            x = _layer(x, p, i, layers[i])
        return x

    return (trunk(x_prompt, p_prompt), trunk(x_sample, p_sample))
```
